```python
import jax
import jax.numpy as jnp
from jax import lax
import numpy as np

D_MODEL = 2048
BATCH = 4
SEQ = 2048
DEPTH = 1

CHUNK = 64
NORM_EPS = 1e-6

RW_HEAD_DIM = 64
RW_WIDTH = D_MODEL // 2
RW_HEADS = RW_WIDTH // RW_HEAD_DIM
RW_DECAY_LORA = 96
RW_A_LORA = 96
RW_GATE_LORA = 256
RW_LN_EPS = 64e-5
RW_COLS = 3 * RW_WIDTH + RW_DECAY_LORA + RW_A_LORA + RW_GATE_LORA

RET_HEADS = 8
RET_QK_DIM = 128
RET_V_DIM = 256
RET_QK_WIDTH = RET_HEADS * RET_QK_DIM
RET_V_WIDTH = RET_HEADS * RET_V_DIM
RET_COLS = 2 * RET_QK_WIDTH + 2 * RET_V_WIDTH
ROPE_BASE = 10000.0

GATE_COLS = 2 * D_MODEL
IN_COLS = RW_COLS + RET_COLS + GATE_COLS

N_GROUPS = 4
EXPERTS_PER_GROUP = 8
N_EXPERTS = N_GROUPS * EXPERTS_PER_GROUP
TOP_K = 2
EXPERT_FF = 512
MOE_BLOCK = 128

kernel_name = 'hybrid_rwkv7_retention_hmoe_block'


def rms_norm(x, g):
    xf = x.astype(jnp.float32)
    y = xf * lax.rsqrt(jnp.mean(xf * xf, axis=-1, keepdims=True) + NORM_EPS)
    return (y * g.astype(jnp.float32)).astype(x.dtype)


def rotary(t, pos):
    half = t.shape[-1] // 2
    inv = ROPE_BASE ** (-jnp.arange(half, dtype=jnp.float32) / half)
    ang = pos[:, None] * inv[None, :]
    cos = jnp.cos(ang)[None, :, None, :]
    sin = jnp.sin(ang)[None, :, None, :]
    t1, t2 = t[..., :half], t[..., half:]
    return jnp.concatenate([t1 * cos - t2 * sin, t1 * sin + t2 * cos], axis=-1)


def rwkv7_recurrence(r, w, k, v, a, b):
    bsz, _, nh, n = r.shape

    def step(state, inp):
        r_t, w_t, k_t, v_t, a_t, b_t = inp
        sa = jnp.einsum('bhvk,bhk->bhv', state, a_t)
        state = (state * w_t[:, :, None, :] + sa[..., None] * b_t[:, :, None, :]
                 + v_t[..., None] * k_t[:, :, None, :])
        return state, jnp.einsum('bhvk,bhk->bhv', state, r_t)

    xs = tuple(jnp.swapaxes(t, 0, 1) for t in (r, w, k, v, a, b))
    state0 = jnp.zeros((bsz, nh, n, n), jnp.float32)
    _, y = lax.scan(step, state0, xs)
    return jnp.swapaxes(y, 0, 1)


def rwkv7_time_mix(p, mu, w0, w2, a0, a2, g2, k_k, k_a, r_k, lnx_g, lnx_b):
    bsz, seq, _ = p.shape
    f32 = jnp.float32
    hd = (RW_HEADS, RW_HEAD_DIM)
    shp = (bsz, seq) + hd
    p_prev = jnp.pad(p[:, :-1], ((0, 0), (1, 0), (0, 0)))
    p = p + (p_prev - p) * mu
    cuts = [RW_WIDTH, 2 * RW_WIDTH, 3 * RW_WIDTH, 3 * RW_WIDTH + RW_DECAY_LORA,
            3 * RW_WIDTH + RW_DECAY_LORA + RW_A_LORA]
    r, k, v, xw, xa, xg = jnp.split(p, cuts, axis=-1)
    w_log = -jax.nn.softplus(-(w0 + jnp.tanh(xw) @ w2).astype(f32)) - 0.5
    decay = jnp.exp(-jnp.exp(w_log)).reshape(shp)
    a = jax.nn.sigmoid((a0 + xa @ a2).astype(f32)).reshape(shp)
    g = (jax.nn.sigmoid(xg) @ g2).astype(f32)
    r_h = r.astype(f32).reshape(shp)
    k_h = k.astype(f32).reshape(shp)
    v_h = v.astype(f32).reshape(shp)
    kk = k_h * k_k.astype(f32).reshape(hd)
    kk = kk * lax.rsqrt(jnp.maximum(jnp.sum(kk * kk, axis=-1, keepdims=True), 1e-24))
    k_h = k_h * (1.0 + (a - 1.0) * k_a.astype(f32).reshape(hd))
    y = rwkv7_recurrence(r_h, decay, k_h, v_h, -kk, kk * a)
    mean = jnp.mean(y, axis=-1, keepdims=True)
    var = jnp.mean(jnp.square(y - mean), axis=-1, keepdims=True)
    y = (y - mean) * lax.rsqrt(var + RW_LN_EPS)
    y = y * lnx_g.astype(f32).reshape(hd) + lnx_b.astype(f32).reshape(hd)
    bonus = jnp.sum(r_h * k_h * r_k.astype(f32).reshape(hd), axis=-1, keepdims=True) * v_h
    y = (y + bonus).reshape(bsz, seq, RW_WIDTH)
    return (y * g).astype(p.dtype)


def retention_mix(p, norm_g):
    bsz, seq, _ = p.shape
    f32 = jnp.float32
    n_chunks = seq // CHUNK
    q, k, v, g = jnp.split(p, [RET_QK_WIDTH, 2 * RET_QK_WIDTH, 2 * RET_QK_WIDTH + RET_V_WIDTH], axis=-1)
    pos = jnp.arange(seq, dtype=f32)
    q = rotary(q.astype(f32).reshape(bsz, seq, RET_HEADS, RET_QK_DIM), pos)
    k = rotary(k.astype(f32).reshape(bsz, seq, RET_HEADS, RET_QK_DIM), pos) * (RET_QK_DIM ** -0.5)
    cs = (bsz, n_chunks, CHUNK, RET_HEADS)
    q = q.reshape(cs + (RET_QK_DIM,))
    k = k.reshape(cs + (RET_QK_DIM,))
    v = v.astype(f32).reshape(cs + (RET_V_DIM,))
    log_gamma = jnp.log1p(-jnp.exp2(-5.0 - jnp.arange(RET_HEADS, dtype=f32)))
    idx = jnp.arange(CHUNK, dtype=f32)
    dist = jnp.abs(idx[:, None] - idx[None, :])
    d_intra = jnp.exp(dist[None] * log_gamma[:, None, None])
    scores = jnp.einsum('bnihd,bnjhd->bnhij', q, k) * d_intra
    o_intra = jnp.einsum('bnhij,bnjhe->bnihe', scores, v)
    k_decay = jnp.exp((CHUNK - 1.0 - idx)[:, None] * log_gamma[None, :])
    kv = jnp.einsum('bnjhd,jh,bnjhe->nbhde', k, k_decay, v)
    chunk_decay = jnp.exp(CHUNK * log_gamma)[:, None, None]

    def step(state, kv_n):
        return state * chunk_decay + kv_n, state

    _, s_prev = lax.scan(step, jnp.zeros(kv.shape[1:], f32), kv)
    q_decay = jnp.exp((idx + 1.0)[:, None] * log_gamma[None, :])
    o_inter = jnp.einsum('bnihd,ih,nbhde->bnihe', q, q_decay, s_prev)
    o = (o_intra + o_inter).reshape(bsz, seq, RET_HEADS, RET_V_DIM)
    o = o * lax.rsqrt(jnp.mean(o * o, axis=-1, keepdims=True) + NORM_EPS)
    o = o.reshape(bsz, seq, RET_V_WIDTH) * norm_g.astype(f32)
    return (o * jax.nn.silu(g.astype(f32))).astype(p.dtype)


def hier_moe(h, w_group, b_group, w_expert, b_expert, e_gate, e_up, e_down):
    bsz, seq, d = h.shape
    f32 = jnp.float32
    n_tok = bsz * seq
    hf = h.reshape(n_tok, d)
    g_logits = (hf @ w_group).astype(f32) + b_group.astype(f32)
    g_prob = jax.nn.softmax(g_logits, axis=-1)
    g_sel = jnp.argmax(g_logits, axis=-1)
    g_w = jnp.take_along_axis(g_prob, g_sel[:, None], axis=-1)
    e_logits = ((hf @ w_expert).astype(f32) + b_expert.astype(f32)).reshape(n_tok, N_GROUPS, EXPERTS_PER_GROUP)
    e_in = jnp.take_along_axis(e_logits, g_sel[:, None, None], axis=1)[:, 0]
    top_l, top_i = lax.top_k(e_in, TOP_K)
    top_w = jax.nn.softmax(top_l, axis=-1) * g_w
    expert_id = (g_sel[:, None] * EXPERTS_PER_GROUP + top_i).reshape(-1).astype(jnp.int32)
    token_id = jnp.repeat(jnp.arange(n_tok, dtype=jnp.int32), TOP_K)
    weight = top_w.reshape(-1)
    n_assign = n_tok * TOP_K
    order = jnp.argsort(expert_id)
    e_sorted = expert_id[order]
    counts = jax.ops.segment_sum(jnp.ones((n_assign,), jnp.int32), expert_id, num_segments=N_EXPERTS)
    starts = jnp.cumsum(counts) - counts
    padded = (counts + MOE_BLOCK - 1) // MOE_BLOCK * MOE_BLOCK
    pad_end = jnp.cumsum(padded)
    pad_start = pad_end - padded
    dest = pad_start[e_sorted] + jnp.arange(n_assign, dtype=jnp.int32) - starts[e_sorted]
    n_blocks = -(-n_assign // MOE_BLOCK) + N_EXPERTS
    n_slots = n_blocks * MOE_BLOCK
    slot_tok = jnp.full((n_slots,), n_tok, jnp.int32).at[dest].set(token_id[order])
    slot_w = jnp.zeros((n_slots,), f32).at[dest].set(weight[order])
    block_e = jnp.minimum(
        jnp.searchsorted(pad_end, jnp.arange(n_blocks, dtype=jnp.int32) * MOE_BLOCK, side='right'),
        N_EXPERTS - 1)
    h_pad = jnp.concatenate([hf, jnp.zeros((1, d), hf.dtype)], axis=0)

    def expert_block(args):
        tok_b, e = args
        xb = h_pad[tok_b]
        return (jax.nn.silu(xb @ e_gate[e]) * (xb @ e_up[e])) @ e_down[e]

    out = lax.map(expert_block, (slot_tok.reshape(n_blocks, MOE_BLOCK), block_e))
    out = out.reshape(n_slots, d).astype(f32) * slot_w[:, None]
    y = jax.ops.segment_sum(out, slot_tok, num_segments=n_tok + 1)[:n_tok]
    return y.reshape(bsz, seq, d).astype(h.dtype)


def setup_inputs(seed: int = 0) -> dict:
    key = jax.random.key(seed)
    ks = jax.random.split(key, 32)
    f32 = jnp.float32
    L = DEPTH

    def nrm(k, shape, scale):
        return jax.random.normal(k, shape, f32) * scale

    def gain(k, shape):
        return 1.0 + 0.01 * jax.random.normal(k, shape, f32)

    return {
        'x': nrm(ks[0], (BATCH, SEQ, D_MODEL), 1.0),
        'norm1_g': gain(ks[1], (L, D_MODEL)),
        'w_in': nrm(ks[2], (L, D_MODEL, IN_COLS), D_MODEL ** -0.5),
        'mu_shift': jax.random.uniform(ks[3], (L, RW_COLS), f32),
        'rw_w0': jax.random.uniform(ks[4], (L, RW_WIDTH), f32, -6.0, 0.0),
        'rw_w2': nrm(ks[5], (L, RW_DECAY_LORA, RW_WIDTH), 0.5 * RW_DECAY_LORA ** -0.5),
        'rw_a0': nrm(ks[6], (L, RW_WIDTH), 0.1),
        'rw_a2': nrm(ks[7], (L, RW_A_LORA, RW_WIDTH), 0.5 * RW_A_LORA ** -0.5),
        'rw_g2': nrm(ks[8], (L, RW_GATE_LORA, RW_WIDTH), RW_GATE_LORA ** -0.5),
        'rw_k_k': 0.85 + nrm(ks[9], (L, RW_WIDTH), 0.05),
        'rw_k_a': 1.0 + nrm(ks[10], (L, RW_WIDTH), 0.05),
        'rw_r_k': nrm(ks[11], (L, RW_WIDTH), 0.1),
        'rw_lnx_g': gain(ks[12], (L, RW_WIDTH)),
        'rw_lnx_b': nrm(ks[13], (L, RW_WIDTH), 0.01),
        'ret_norm_g': gain(ks[14], (L, RET_V_WIDTH)),
        'w_up_a': nrm(ks[15], (L, RW_WIDTH, D_MODEL), RW_WIDTH ** -0.5),
        'w_up_b': nrm(ks[16], (L, RET_V_WIDTH, D_MODEL), RET_V_WIDTH ** -0.5),
        'w_out': nrm(ks[17], (L, D_MODEL, D_MODEL), D_MODEL ** -0.5),
        'norm2_g': gain(ks[18], (L, D_MODEL)),
        'w_group': nrm(ks[19], (L, D_MODEL, N_GROUPS), D_MODEL ** -0.5),
        'b_group': nrm(ks[20], (L, N_GROUPS), 0.01),
        'w_expert': nrm(ks[21], (L, D_MODEL, N_EXPERTS), D_MODEL ** -0.5),
        'b_expert': nrm(ks[22], (L, N_EXPERTS), 0.01),
        'e_gate': nrm(ks[23], (L, N_EXPERTS, D_MODEL, EXPERT_FF), D_MODEL ** -0.5),
        'e_up': nrm(ks[24], (L, N_EXPERTS, D_MODEL, EXPERT_FF), D_MODEL ** -0.5),
        'e_down': nrm(ks[25], (L, N_EXPERTS, EXPERT_FF, D_MODEL), EXPERT_FF ** -0.5),
        'final_norm_g': gain(ks[26], (D_MODEL,)),
    }


def reference(x, norm1_g, w_in, mu_shift, rw_w0, rw_w2, rw_a0, rw_a2, rw_g2, rw_k_k, rw_k_a,
              rw_r_k, rw_lnx_g, rw_lnx_b, ret_norm_g, w_up_a, w_up_b, w_out, norm2_g,
              w_group, b_group, w_expert, b_expert, e_gate, e_up, e_down, final_norm_g):
    for l in range(DEPTH):
        h = rms_norm(x, norm1_g[l])
        p = h @ w_in[l]
        p_rw, p_ret, p_gate = jnp.split(p, [RW_COLS, RW_COLS + RET_COLS], axis=-1)
        o_a = rwkv7_time_mix(p_rw, mu_shift[l], rw_w0[l], rw_w2[l], rw_a0[l], rw_a2[l], rw_g2[l],
                             rw_k_k[l], rw_k_a[l], rw_r_k[l], rw_lnx_g[l], rw_lnx_b[l])
        o_b = retention_mix(p_ret, ret_norm_g[l])
        gate_a, gate_b = jnp.split(jax.nn.sigmoid(p_gate), 2, axis=-1)
        merged = gate_a * (o_a @ w_up_a[l]) + gate_b * (o_b @ w_up_b[l])
        x = x + merged @ w_out[l]
        x = x + hier_moe(rms_norm(x, norm2_g[l]), w_group[l], b_group[l], w_expert[l], b_expert[l],
                         e_gate[l], e_up[l], e_down[l])
    return rms_norm(x, final_norm_g)
```

```python
import functools

import jax
import jax.numpy as jnp
import numpy as np
from jax import lax
from jax.experimental import pallas as pl
from jax.experimental.pallas import tpu as pltpu

F32 = jnp.float32
BF16 = jnp.bfloat16

D_MODEL = 2048
NORM_EPS = 1e-6

RW_HEAD_DIM = 64
RW_WIDTH = 1024
RW_DECAY_LORA = 96
RW_A_LORA = 96
RW_GATE_LORA = 256
RW_LN_EPS = 64e-5
RW_COLS = 3 * RW_WIDTH + RW_DECAY_LORA + RW_A_LORA + RW_GATE_LORA
RW_COLS_PAD = 3584
RW_LORA_COLS = RW_COLS_PAD - 3 * RW_WIDTH

RET_HEADS = 8
RET_QK_DIM = 128
RET_V_DIM = 256
RET_QK_WIDTH = RET_HEADS * RET_QK_DIM
RET_V_WIDTH = RET_HEADS * RET_V_DIM
RET_COLS = 2 * RET_QK_WIDTH + 2 * RET_V_WIDTH
ROPE_BASE = 10000.0

GATE_COLS = 2 * D_MODEL
P_COLS = RW_COLS_PAD + RET_COLS + GATE_COLS
RET_OFF = RW_COLS_PAD
GATE_OFF = RW_COLS_PAD + RET_COLS

N_GROUPS = 4
EXPERTS_PER_GROUP = 8
N_EXPERTS = 32
TOP_K = 2
EXPERT_FF = 512
MOE_BLOCK = 128
ROUTER_COLS = 128

RW_CHUNK = 64
RW_TILE = 256
RET_TILE = 256
VMEM_LIMIT = 56 * 1024 * 1024


def _dot(a, b):
    return jnp.dot(a.astype(BF16), b.astype(BF16), preferred_element_type=F32)


def _dot_nt(a, b):
    return lax.dot_general(a.astype(BF16), b.astype(BF16), (((1,), (1,)), ((), ())),
                           preferred_element_type=F32)


def _dot_exact_rhs(x, m_bf16):
    x1 = x.astype(BF16)
    r1 = x - x1.astype(F32)
    x2 = r1.astype(BF16)
    x3 = (r1 - x2.astype(F32)).astype(BF16)
    d = lambda t: jnp.dot(t, m_bf16, preferred_element_type=F32)
    return d(x1) + d(x2) + d(x3)


def _dot_exact_lhs(m_bf16, x):
    x1 = x.astype(BF16)
    r1 = x - x1.astype(F32)
    x2 = r1.astype(BF16)
    x3 = (r1 - x2.astype(F32)).astype(BF16)
    d = lambda t: jnp.dot(m_bf16, t, preferred_element_type=F32)
    return d(x1) + d(x2) + d(x3)


def _in_proj_kernel(x_ref, g_ref, w_ref, o_ref, h_ref):
    @pl.when(pl.program_id(1) == 0)
    def _():
        x = x_ref[...]
        ms = jnp.mean(x * x, axis=-1, keepdims=True)
        h_ref[...] = (x * lax.rsqrt(ms + NORM_EPS) * g_ref[...]).astype(BF16)

    o_ref[...] = jnp.dot(h_ref[...], w_ref[...], preferred_element_type=F32)


def _in_proj(x2d, g, w_bf16, tm=512, tn=512):
    n_tok = x2d.shape[0]
    return pl.pallas_call(
        _in_proj_kernel,
        out_shape=jax.ShapeDtypeStruct((n_tok, P_COLS), F32),
        grid=(n_tok // tm, P_COLS // tn),
        in_specs=[
            pl.BlockSpec((tm, D_MODEL), lambda i, j: (i, 0)),
            pl.BlockSpec((1, D_MODEL), lambda i, j: (0, 0)),
            pl.BlockSpec((D_MODEL, tn), lambda i, j: (0, j)),
        ],
        out_specs=pl.BlockSpec((tm, tn), lambda i, j: (i, j)),
        scratch_shapes=[pltpu.VMEM((tm, D_MODEL), BF16)],
        compiler_params=pltpu.CompilerParams(
            dimension_semantics=("parallel", "arbitrary"), vmem_limit_bytes=VMEM_LIMIT),
        name="in_proj",
    )(x2d, g, w_bf16)


_VEC_ROWS = 16


def _rwkv_kernel(pr_ref, pk_ref, pv_ref, pl_ref, pr8_ref, pk8_ref, pv8_ref, pl8_ref,
                 vec_ref, mul_ref, wl_ref, o_ref, st_ref):
    s = pl.program_id(2)
    tile, c = RW_TILE, RW_CHUNK
    n_chunks = tile // c

    @pl.when(s == 0)
    def _():
        st_ref[...] = jnp.zeros_like(st_ref)

    first = (s == 0)
    row0 = lax.broadcasted_iota(jnp.int32, (tile, 1), 0) == 0

    def shift_mix(cur_ref, prev_ref, mu):
        cur = cur_ref[...]
        prev_row = jnp.where(first, 0.0, prev_ref[7:8, :])
        prev = jnp.where(row0, prev_row, pltpu.roll(cur, 1, 0))
        return cur + (prev - cur) * mu

    vec = vec_ref[...]
    mu_r, mu_k, mu_v = vec[0:1], vec[1:2], vec[2:3]
    w0, a0 = vec[3:4], vec[4:5]
    k_k, k_a, r_k = vec[5:6], vec[6:7], vec[7:8]
    ln_g, ln_b = vec[8:9], vec[9:10]

    r = shift_mix(pr_ref, pr8_ref, mu_r)
    k = shift_mix(pk_ref, pk8_ref, mu_k)
    v = shift_mix(pv_ref, pv8_ref, mu_v)
    lora = shift_mix(pl_ref, pl8_ref, mul_ref[...])
    col = lax.broadcasted_iota(jnp.int32, (1, 256), 1)
    lo = lora[:, 0:256]
    act_lo = jnp.where(col < RW_DECAY_LORA, jnp.tanh(lo),
                       jnp.where(col < RW_DECAY_LORA + RW_A_LORA, lo, jax.nn.sigmoid(lo)))
    act = jnp.concatenate([act_lo, jax.nn.sigmoid(lora[:, 256:RW_LORA_COLS])], axis=1)
    zag = _dot(act, wl_ref[0])
    nz = -(w0 + zag[:, 0:128])
    softplus = jnp.maximum(nz, 0.0) + jnp.log1p(jnp.exp(-jnp.abs(nz)))
    lw = -jnp.exp(-softplus - 0.5)
    a_gate = jax.nn.sigmoid(a0 + zag[:, 128:256])
    g = zag[:, 256:384]

    lane = lax.broadcasted_iota(jnp.int32, (1, 128), 1)
    m0 = lane < RW_HEAD_DIM
    ri = lax.broadcasted_iota(jnp.int32, (128, 128), 0)
    ci = lax.broadcasted_iota(jnp.int32, (128, 128), 1)
    same_head = (ri // RW_HEAD_DIM) == (ci // RW_HEAD_DIM)
    ones2 = jnp.where(same_head, 1.0, 0.0).astype(BF16)
    tri_s = jnp.where(same_head & (ci < ri), 1.0, 0.0)
    tri_i = jnp.where(same_head & (ci <= ri), 1.0, 0.0)
    eye = jnp.where(ri == ci, 1.0, 0.0)

    kk = k * k_k
    ss = _dot_exact_rhs(kk * kk, ones2)
    kk = kk * lax.rsqrt(jnp.maximum(ss, 1e-24))
    k = k * (1.0 + (a_gate - 1.0) * k_a)
    a_vec = -kk
    b_vec = kk * a_gate

    ti = lax.broadcasted_iota(jnp.int32, (tile, tile), 0)
    tj = lax.broadcasted_iota(jnp.int32, (tile, tile), 1)
    cum_mat = jnp.where((ti // c == tj // c) & (tj <= ti), 1.0, 0.0).astype(BF16)
    cum = _dot_exact_lhs(cum_mat, lw)

    def bd(x):
        return jnp.concatenate([jnp.where(m0, x, 0.0), jnp.where(m0, 0.0, x)], axis=0)

    st = st_ref[...]
    ys = []
    for ch in range(n_chunks):
        sl = slice(ch * c, (ch + 1) * c)
        cum_c, lw_c = cum[sl], lw[sl]
        r_c, k_c, v_c, a_c, b_c = r[sl], k[sl], v[sl], a_vec[sl], b_vec[sl]
        cum_end = cum_c[c - 1:c]
        e_p = jnp.exp(cum_c)
        e_m = jnp.exp(-cum_c)
        e_h = jnp.exp(cum_end - cum_c)
        a_t = a_c * jnp.exp(cum_c - lw_c)
        r_t = r_c * e_p
        b_t, k_t = b_c * e_m, k_c * e_m
        b_h, k_h = b_c * e_h, k_c * e_h

        a_bd, r_bd, v_bd = bd(a_t), bd(r_t), bd(v_c)
        gm = _dot_nt(jnp.concatenate([a_bd, r_bd], axis=0),
                     jnp.concatenate([b_t, b_t, k_t, k_t], axis=0))
        a_ab = gm[0:128, 0:128] * tri_s
        a_ak = gm[0:128, 128:256] * tri_s
        a_rb = gm[128:256, 0:128] * tri_i
        a_rk = gm[128:256, 128:256] * tri_i

        t_inv = eye + a_ab
        l_pow = a_ab
        for _ in range(5):
            l_pow = _dot(l_pow, l_pow)
            t_inv = t_inv + _dot(t_inv, l_pow)

        zz = _dot(t_inv, jnp.concatenate([_dot(a_ak, v_bd), a_bd], axis=1))
        w_bd, ta_bd = zz[:, 0:128], zz[:, 128:256]
        qq = _dot(a_rb, zz)
        yl_bd = qq[:, 0:128] + _dot(a_rk, v_bd)
        rp_bd = r_bd + qq[:, 128:256]
        bh_t = bd(b_h).T
        mn = _dot(bh_t, zz)
        m_t = mn[:, 128:256] + eye * jnp.exp(cum_end)
        n_t = mn[:, 0:128] + _dot(bd(k_h).T, v_bd)

        rp = rp_bd[0:c] + rp_bd[c:2 * c]
        yl = yl_bd[0:c] + yl_bd[c:2 * c]
        ys.append(_dot(rp, st) + yl)
        st = _dot(m_t, st) + n_t
    st_ref[...] = st
    y = jnp.concatenate(ys, axis=0)

    inv_n = 1.0 / RW_HEAD_DIM
    mean = _dot_exact_rhs(y, ones2) * inv_n
    d = y - mean
    var = _dot_exact_rhs(d * d, ones2) * inv_n
    yn = d * lax.rsqrt(var + RW_LN_EPS) * ln_g + ln_b
    bonus = _dot_exact_rhs(r * k * r_k, ones2) * v
    o_ref[...] = ((yn + bonus) * g).astype(o_ref.dtype)


def _lora_weights(w2, a2, g2):
    n_pairs = RW_WIDTH // 128
    w = jnp.zeros((n_pairs, RW_LORA_COLS, 384), F32)
    r0, r1, r2 = RW_DECAY_LORA, RW_DECAY_LORA + RW_A_LORA, RW_DECAY_LORA + RW_A_LORA + RW_GATE_LORA
    w = w.at[:, 0:r0, 0:128].set(w2.reshape(r0, n_pairs, 128).transpose(1, 0, 2))
    w = w.at[:, r0:r1, 128:256].set(a2.reshape(r1 - r0, n_pairs, 128).transpose(1, 0, 2))
    w = w.at[:, r1:r2, 256:384].set(g2.reshape(r2 - r1, n_pairs, 128).transpose(1, 0, 2))
    return w.astype(BF16)


def _rwkv_mix(p_all, vec, mu_lora, w_lora, batch, seq):
    n_tok = batch * seq
    tile = RW_TILE
    steps = seq // tile
    n_pairs = RW_WIDTH // 128
    lora_blk = 3 * RW_WIDTH // RW_LORA_COLS

    def cur(col0):
        return lambda b, p, s: (b * steps + s, col0 + p)

    def prev(col0):
        return lambda b, p, s: (jnp.maximum((b * steps + s) * (tile // 8) - 1, 0), col0 + p)

    in_specs = [
        pl.BlockSpec((tile, 128), cur(0)),
        pl.BlockSpec((tile, 128), cur(RW_WIDTH // 128)),
        pl.BlockSpec((tile, 128), cur(2 * RW_WIDTH // 128)),
        pl.BlockSpec((tile, RW_LORA_COLS), lambda b, p, s: (b * steps + s, lora_blk)),
        pl.BlockSpec((8, 128), prev(0)),
        pl.BlockSpec((8, 128), prev(RW_WIDTH // 128)),
        pl.BlockSpec((8, 128), prev(2 * RW_WIDTH // 128)),
        pl.BlockSpec((8, RW_LORA_COLS),
                     lambda b, p, s: (jnp.maximum((b * steps + s) * (tile // 8) - 1, 0), lora_blk)),
        pl.BlockSpec((_VEC_ROWS, 128), lambda b, p, s: (0, p)),
        pl.BlockSpec((1, RW_LORA_COLS), lambda b, p, s: (0, 0)),
        pl.BlockSpec((1, RW_LORA_COLS, 384), lambda b, p, s: (p, 0, 0)),
    ]
    return pl.pallas_call(
        _rwkv_kernel,
        out_shape=jax.ShapeDtypeStruct((n_tok, RW_WIDTH), BF16),
        grid=(batch, n_pairs, steps),
        in_specs=in_specs,
        out_specs=pl.BlockSpec((tile, 128), lambda b, p, s: (b * steps + s, p)),
        scratch_shapes=[pltpu.VMEM((128, 128), F32)],
        compiler_params=pltpu.CompilerParams(
            dimension_semantics=("parallel", "parallel", "arbitrary"), vmem_limit_bytes=VMEM_LIMIT),
        name="rwkv_mix",
    )(p_all, p_all, p_all, p_all, p_all, p_all, p_all, p_all, vec, mu_lora, w_lora)


def _retention_kernel(q_ref, k_ref, v_ref, g_ref, cos_ref, sin_ref, dmask_ref, qdec_ref, kdec_ref,
                      sdec_ref, ng_ref, o_ref, st_ref):
    @pl.when(pl.program_id(2) == 0)
    def _():
        st_ref[...] = jnp.zeros_like(st_ref)

    cos, sin = cos_ref[...], sin_ref[...]

    def rot(t):
        return t * cos + pltpu.roll(t, RET_QK_DIM // 2, 1) * sin

    q = rot(q_ref[...])
    k = rot(k_ref[...])
    v = v_ref[...].astype(BF16)
    st = st_ref[...]
    scores = _dot_nt(q, k) * dmask_ref[0]
    o = _dot(scores, v) + _dot(q * qdec_ref[0], st)
    st_ref[...] = st * sdec_ref[0] + _dot((k * kdec_ref[0]).T, v)
    o = o * lax.rsqrt(jnp.mean(o * o, axis=-1, keepdims=True) + NORM_EPS) * ng_ref[...]
    gate = g_ref[...]
    o_ref[...] = (o * (gate * jax.nn.sigmoid(gate))).astype(o_ref.dtype)


def _retention_tables(seq):
    tile = RET_TILE
    half = RET_QK_DIM // 2
    pos = jnp.arange(seq, dtype=F32)
    inv = ROPE_BASE ** (-jnp.arange(half, dtype=F32) / half)
    ang = pos[:, None] * inv[None, :]
    cos, sin = jnp.cos(ang), jnp.sin(ang)
    cos2 = jnp.concatenate([cos, cos], axis=-1)
    sin2 = jnp.concatenate([-sin, sin], axis=-1)
    log_gamma = jnp.log1p(-jnp.exp2(-5.0 - jnp.arange(RET_HEADS, dtype=F32)))
    idx = jnp.arange(tile, dtype=F32)
    chunk = jnp.arange(tile) // 64
    visible = chunk[None, :] <= chunk[:, None]
    dist = jnp.abs(idx[:, None] - idx[None, :])
    scale = RET_QK_DIM ** -0.5
    dmask = jnp.where(visible[None], jnp.exp(dist[None] * log_gamma[:, None, None]), 0.0) * scale
    qdec = jnp.exp((idx + 1.0)[None, :, None] * log_gamma[:, None, None])
    kdec = jnp.exp((tile - 1.0 - idx)[None, :, None] * log_gamma[:, None, None]) * scale
    sdec = jnp.exp(tile * log_gamma)[:, None, None]
    ones = jnp.ones((1, 1, 128), F32)
    return cos2, sin2, dmask, qdec * ones, kdec * ones, sdec * jnp.ones((1, 1, RET_V_DIM), F32)


def _retention(p_all, norm_g, batch, seq):
    n_tok = batch * seq
    tile = RET_TILE
    steps = seq // tile
    cos2, sin2, dmask, qdec, kdec, sdec = _retention_tables(seq)
    q0 = RET_OFF // 128
    k0 = (RET_OFF + RET_QK_WIDTH) // 128
    v0 = (RET_OFF + 2 * RET_QK_WIDTH) // 256
    g0 = (RET_OFF + 2 * RET_QK_WIDTH + RET_V_WIDTH) // 256
    row = lambda b, h, s: b * steps + s
    in_specs = [
        pl.BlockSpec((tile, 128), lambda b, h, s: (row(b, h, s), q0 + h)),
        pl.BlockSpec((tile, 128), lambda b, h, s: (row(b, h, s), k0 + h)),
        pl.BlockSpec((tile, 256), lambda b, h, s: (row(b, h, s), v0 + h)),
        pl.BlockSpec((tile, 256), lambda b, h, s: (row(b, h, s), g0 + h)),
        pl.BlockSpec((tile, 128), lambda b, h, s: (s, 0)),
        pl.BlockSpec((tile, 128), lambda b, h, s: (s, 0)),
        pl.BlockSpec((1, tile, tile), lambda b, h, s: (h, 0, 0)),
        pl.BlockSpec((1, tile, 128), lambda b, h, s: (h, 0, 0)),
        pl.BlockSpec((1, tile, 128), lambda b, h, s: (h, 0, 0)),
        pl.BlockSpec((1, 1, RET_V_DIM), lambda b, h, s: (h, 0, 0)),
        pl.BlockSpec((1, 256), lambda b, h, s: (0, h)),
    ]
    return pl.pallas_call(
        _retention_kernel,
        out_shape=jax.ShapeDtypeStruct((n_tok, RET_V_WIDTH), BF16),
        grid=(batch, RET_HEADS, steps),
        in_specs=in_specs,
        out_specs=pl.BlockSpec((tile, 256), lambda b, h, s: (row(b, h, s), h)),
        scratch_shapes=[pltpu.VMEM((RET_QK_DIM, RET_V_DIM), F32)],
        compiler_params=pltpu.CompilerParams(
            dimension_semantics=("parallel", "parallel", "arbitrary"), vmem_limit_bytes=VMEM_LIMIT),
        name="retention",
    )(p_all, p_all, p_all, p_all, cos2, sin2, dmask, qdec, kdec, sdec, norm_g)


def _merge_kernel(oa_ref, ob_ref, ga_ref, gb_ref, wa_ref, wb_ref, wo_ref, x_ref, n2_ref, wr_ref,
                  x1_ref, h2_ref, lg_ref, m_ref):
    j = pl.program_id(1)
    tn = wa_ref.shape[1]
    ya = jnp.dot(oa_ref[...], wa_ref[...], preferred_element_type=F32)
    yb = jnp.dot(ob_ref[...], wb_ref[...], preferred_element_type=F32)
    merged = jax.nn.sigmoid(ga_ref[...]) * ya + jax.nn.sigmoid(gb_ref[...]) * yb
    m_ref[j] = merged.astype(BF16)

    @pl.when(j == pl.num_programs(1) - 1)
    def _():
        acc = x_ref[...]
        for jj in range(D_MODEL // tn):
            acc = acc + jnp.dot(m_ref[jj], wo_ref[jj * tn:(jj + 1) * tn, :],
                                preferred_element_type=F32)
        x1_ref[...] = acc
        ms = jnp.mean(acc * acc, axis=-1, keepdims=True)
        h2 = acc * lax.rsqrt(ms + NORM_EPS) * n2_ref[...]
        h2_ref[...] = h2
        lg_ref[...] = _dot_exact_rhs_f32(h2, wr_ref[...])


def _dot_exact_rhs_f32(x, w):
    def split(t):
        t1 = t.astype(BF16)
        r1 = t - t1.astype(F32)
        t2 = r1.astype(BF16)
        t3 = (r1 - t2.astype(F32)).astype(BF16)
        return t1, t2, t3
    x1, x2, x3 = split(x)
    w1, w2, w3 = split(w)
    d = lambda a, b: jnp.dot(a, b, preferred_element_type=F32)
    return (d(x1, w1) + (d(x1, w2) + d(x2, w1)) + (d(x2, w2) + d(x1, w3) + d(x3, w1)))


def _merge_out(o_a, o_b, p_all, w_up_a, w_up_b, w_out, x2d, norm2_g, w_router, tm=256, tn=512):
    n_tok = x2d.shape[0]
    nj = D_MODEL // tn
    ga0 = GATE_OFF // tn
    gb0 = (GATE_OFF + D_MODEL) // tn
    once = pl.Buffered(1)
    return pl.pallas_call(
        _merge_kernel,
        out_shape=(jax.ShapeDtypeStruct((n_tok, D_MODEL), F32),
                   jax.ShapeDtypeStruct((n_tok, D_MODEL), F32),
                   jax.ShapeDtypeStruct((n_tok, ROUTER_COLS), F32)),
        grid=(n_tok // tm, nj),
        in_specs=[
            pl.BlockSpec((tm, RW_WIDTH), lambda i, j: (i, 0)),
            pl.BlockSpec((tm, RET_V_WIDTH), lambda i, j: (i, 0)),
            pl.BlockSpec((tm, tn), lambda i, j: (i, ga0 + j)),
            pl.BlockSpec((tm, tn), lambda i, j: (i, gb0 + j)),
            pl.BlockSpec((RW_WIDTH, tn), lambda i, j: (0, j)),
            pl.BlockSpec((RET_V_WIDTH, tn), lambda i, j: (0, j)),
            pl.BlockSpec((D_MODEL, D_MODEL), lambda i, j: (0, 0), pipeline_mode=once),
            pl.BlockSpec((tm, D_MODEL), lambda i, j: (i, 0)),
            pl.BlockSpec((1, D_MODEL), lambda i, j: (0, 0)),
            pl.BlockSpec((D_MODEL, ROUTER_COLS), lambda i, j: (0, 0), pipeline_mode=once),
        ],
        out_specs=(pl.BlockSpec((tm, D_MODEL), lambda i, j: (i, 0)),
                   pl.BlockSpec((tm, D_MODEL), lambda i, j: (i, 0)),
                   pl.BlockSpec((tm, ROUTER_COLS), lambda i, j: (i, 0))),
        scratch_shapes=[pltpu.VMEM((nj, tm, tn), BF16)],
        compiler_params=pltpu.CompilerParams(
            dimension_semantics=("parallel", "arbitrary"), vmem_limit_bytes=VMEM_LIMIT),
        name="merge_out",
    )(o_a, o_b, p_all, p_all, w_up_a, w_up_b, w_out, x2d, norm2_g, w_router)


def _expert_kernel(be_ref, nb_ref, xs_ref, wg_ref, wu_ref, wd_ref, o_ref):
    i = pl.program_id(0)

    @pl.when(i < nb_ref[0])
    def _():
        xb = xs_ref[...].astype(BF16)
        hg = jnp.dot(xb, wg_ref[0], preferred_element_type=F32)
        hu = jnp.dot(xb, wu_ref[0], preferred_element_type=F32)
        act = (hg * jax.nn.sigmoid(hg) * hu).astype(BF16)
        o_ref[...] = jnp.dot(act, wd_ref[0], preferred_element_type=F32)

    @pl.when(i >= nb_ref[0])
    def _():
        o_ref[...] = jnp.zeros_like(o_ref)


def _experts(block_e, n_used, xs, e_gate, e_up, e_down):
    n_slots = xs.shape[0]
    n_blocks = n_slots // MOE_BLOCK
    grid_spec = pltpu.PrefetchScalarGridSpec(
        num_scalar_prefetch=2,
        grid=(n_blocks,),
        in_specs=[
            pl.BlockSpec((MOE_BLOCK, D_MODEL), lambda i, be, nb: (i, 0)),
            pl.BlockSpec((1, D_MODEL, EXPERT_FF), lambda i, be, nb: (be[i], 0, 0)),
            pl.BlockSpec((1, D_MODEL, EXPERT_FF), lambda i, be, nb: (be[i], 0, 0)),
            pl.BlockSpec((1, EXPERT_FF, D_MODEL), lambda i, be, nb: (be[i], 0, 0)),
        ],
        out_specs=pl.BlockSpec((MOE_BLOCK, D_MODEL), lambda i, be, nb: (i, 0)),
    )
    return pl.pallas_call(
        _expert_kernel,
        out_shape=jax.ShapeDtypeStruct((n_slots, D_MODEL), F32),
        grid_spec=grid_spec,
        compiler_params=pltpu.CompilerParams(
            dimension_semantics=("arbitrary",), vmem_limit_bytes=VMEM_LIMIT),
        name="experts",
    )(block_e, n_used, xs, e_gate, e_up, e_down)


def _final_kernel(x1_ref, y0_ref, y1_ref, w_ref, g_ref, o_ref, *, normalize):
    w = w_ref[...]
    x = x1_ref[...] + (y0_ref[...] * w[:, 0:1] + y1_ref[...] * w[:, 1:2])
    if normalize:
        ms = jnp.mean(x * x, axis=-1, keepdims=True)
        x = x * lax.rsqrt(ms + NORM_EPS) * g_ref[...]
    o_ref[...] = x


def _final_norm(x1, y0, y1, w, g, tm=512):
    n_tok = x1.shape[0]
    spec = pl.BlockSpec((tm, D_MODEL), lambda i: (i, 0))
    normalize = g is not None
    g = jnp.ones((1, D_MODEL), F32) if g is None else g[None, :]
    return pl.pallas_call(
        functools.partial(_final_kernel, normalize=normalize),
        out_shape=jax.ShapeDtypeStruct((n_tok, D_MODEL), F32),
        grid=(n_tok // tm,),
        in_specs=[spec, spec, spec, pl.BlockSpec((tm, TOP_K), lambda i: (i, 0)),
                  pl.BlockSpec((1, D_MODEL), lambda i: (0, 0))],
        out_specs=spec,
        compiler_params=pltpu.CompilerParams(
            dimension_semantics=("parallel",), vmem_limit_bytes=VMEM_LIMIT),
        name="final_norm",
    )(x1, y0, y1, w, g)


def _route(logits, b_group, b_expert):
    n_tok = logits.shape[0]
    g_logits = logits[:, :N_GROUPS] + b_group
    e_logits = (logits[:, N_GROUPS:N_GROUPS + N_EXPERTS] + b_expert).reshape(
        n_tok, N_GROUPS, EXPERTS_PER_GROUP)
    g_prob = jax.nn.softmax(g_logits, axis=-1)
    g_sel = jnp.argmax(g_logits, axis=-1)
    g_w = jnp.take_along_axis(g_prob, g_sel[:, None], axis=-1)
    e_in = jnp.take_along_axis(e_logits, g_sel[:, None, None], axis=1)[:, 0]
    top_l, top_i = lax.top_k(e_in, TOP_K)
    top_w = jax.nn.softmax(top_l, axis=-1) * g_w
    expert_id = (g_sel[:, None] * EXPERTS_PER_GROUP + top_i).reshape(-1).astype(jnp.int32)
    weight = top_w.reshape(-1)

    n_assign = n_tok * TOP_K
    onehot = (expert_id[:, None] == jnp.arange(N_EXPERTS, dtype=jnp.int32)[None, :]).astype(jnp.int32)
    rank = jnp.take_along_axis(jnp.cumsum(onehot, axis=0), expert_id[:, None], axis=1)[:, 0] - 1
    counts = jnp.sum(onehot, axis=0)
    padded = (counts + MOE_BLOCK - 1) // MOE_BLOCK * MOE_BLOCK
    pad_end = jnp.cumsum(padded)
    pad_start = pad_end - padded
    dest = pad_start[expert_id] + rank
    n_blocks = -(-n_assign // MOE_BLOCK) + N_EXPERTS
    n_slots = n_blocks * MOE_BLOCK
    token_id = jnp.arange(n_assign, dtype=jnp.int32) // TOP_K
    slot_tok = jnp.zeros((n_slots,), jnp.int32).at[dest].set(token_id)
    block_e = jnp.minimum(
        jnp.searchsorted(pad_end, jnp.arange(n_blocks, dtype=jnp.int32) * MOE_BLOCK, side='right'),
        N_EXPERTS - 1).astype(jnp.int32)
    n_used = (pad_end[-1] // MOE_BLOCK).astype(jnp.int32).reshape(1)
    return slot_tok, top_w, block_e, n_used, dest.reshape(n_tok, TOP_K)


def kernel(x, norm1_g, w_in, mu_shift, rw_w0, rw_w2, rw_a0, rw_a2, rw_g2, rw_k_k, rw_k_a, rw_r_k,
           rw_lnx_g, rw_lnx_b, ret_norm_g, w_up_a, w_up_b, w_out, norm2_g, w_group, b_group,
           w_expert, b_expert, e_gate, e_up, e_down, final_norm_g):
    batch, seq, _ = x.shape
    n_tok = batch * seq
    depth = w_in.shape[0]
    x2d = x.reshape(n_tok, D_MODEL)
    for l in range(depth):
        w_pad = jnp.concatenate(
            [w_in[l][:, :RW_COLS], jnp.zeros((D_MODEL, RW_COLS_PAD - RW_COLS), F32),
             w_in[l][:, RW_COLS:]], axis=1).astype(BF16)
        p_all = _in_proj(x2d, norm1_g[l][None, :], w_pad)

        mu = mu_shift[l]
        rows = [mu[0:RW_WIDTH], mu[RW_WIDTH:2 * RW_WIDTH], mu[2 * RW_WIDTH:3 * RW_WIDTH],
                rw_w0[l], rw_a0[l], rw_k_k[l], rw_k_a[l], rw_r_k[l], rw_lnx_g[l], rw_lnx_b[l]]
        vec = jnp.concatenate(
            [jnp.stack(rows), jnp.zeros((_VEC_ROWS - len(rows), RW_WIDTH), F32)], axis=0)
        mu_lora = jnp.concatenate(
            [mu[3 * RW_WIDTH:], jnp.zeros((RW_COLS_PAD - RW_COLS,), F32)])[None, :]
        o_a = _rwkv_mix(p_all, vec, mu_lora, _lora_weights(rw_w2[l], rw_a2[l], rw_g2[l]), batch, seq)
        o_b = _retention(p_all, ret_norm_g[l][None, :], batch, seq)

        w_router = jnp.concatenate(
            [w_group[l], w_expert[l],
             jnp.zeros((D_MODEL, ROUTER_COLS - N_GROUPS - N_EXPERTS), F32)], axis=1)
        x1, h2, logits = _merge_out(o_a, o_b, p_all, w_up_a[l].astype(BF16), w_up_b[l].astype(BF16),
                                    w_out[l].astype(BF16), x2d, norm2_g[l][None, :], w_router)

        slot_tok, top_w, block_e, n_used, dest = _route(logits, b_group[l], b_expert[l])
        xs = h2[slot_tok]
        out = _experts(block_e, n_used, xs, e_gate[l].astype(BF16),
                       e_up[l].astype(BF16), e_down[l].astype(BF16))
        y0 = out[dest[:, 0]]
        y1 = out[dest[:, 1]]
        g_out = final_norm_g if l == depth - 1 else None
        x2d = _final_norm(x1, y0, y1, top_w, g_out)
    return x2d.reshape(batch, seq, D_MODEL)
```
